```python
import jax, jax.numpy as jnp
from jax import lax
import numpy as np

D_MODEL = 1024
BATCH = 16
SEQ = 2048
DEPTH = 1
DEC_BATCH = 16
DEC_SEQ = 64
PAST_LEN = 1024

CHUNK = 64
D_MIX = 2 * D_MODEL
D_POOL = D_MIX // 2
POOL_WINDOWS = (2, 4, 8, 16)
N_POOL_GROUPS = len(POOL_WINDOWS)
POOL_GROUP = D_POOL // N_POOL_GROUPS
POOL_HIST = max(POOL_WINDOWS) - 1
D_GLA_V = D_MIX - D_POOL
D_GLA_K = D_GLA_V // 2
N_GLA_HEADS = 4
HEAD_K = D_GLA_K // N_GLA_HEADS
HEAD_V = D_GLA_V // N_GLA_HEADS
GATE_RANK = 16
GATE_NORM = 16.0
GLA_CHUNK = CHUNK
EPS = 1e-6
D_IN_PROJ = 2 * D_POOL + 2 * D_GLA_K + 2 * D_GLA_V + GATE_RANK

kernel_name = "hybrid_pool_gla_stream_step"


def _rmsnorm(x, g):
    xf = x.astype(jnp.float32)
    y = xf * lax.rsqrt(jnp.mean(xf * xf, axis=-1, keepdims=True) + EPS)
    return y * g.astype(jnp.float32)


def _pool_mix(u_ext, pos0, w_pool, pool_scale):
    B, L, _ = u_ext.shape
    T = L - POOL_HIST
    uf = u_ext.astype(jnp.float32)
    cs = jnp.concatenate([jnp.zeros((B, 1, D_POOL), jnp.float32), jnp.cumsum(uf, axis=1)], axis=1)
    end = cs[:, POOL_HIST + 1:]
    u_t = uf[:, POOL_HIST:]
    pos = pos0 + jnp.arange(T, dtype=jnp.int32)
    parts = []
    for gi, w in enumerate(POOL_WINDOWS):
        sl = slice(gi * POOL_GROUP, (gi + 1) * POOL_GROUP)
        start = cs[:, POOL_HIST + 1 - w: POOL_HIST + 1 - w + T, sl]
        cnt = jnp.minimum(w, pos + 1).astype(jnp.float32)[None, :, None]
        parts.append((end[..., sl] - start) / cnt - u_t[..., sl])
    p = jnp.stack(parts, axis=2)
    y = jnp.einsum('btgc,gcd->btgd', p, w_pool.astype(jnp.float32)).reshape(B, T, D_POOL)
    return y * pool_scale.astype(jnp.float32)


def _gla(q, k, v, log_a, s0):
    B, T, H, _ = q.shape
    n = -(-T // GLA_CHUNK)
    pad = n * GLA_CHUNK - T

    def blocks(a):
        a = jnp.pad(a, ((0, 0), (0, pad), (0, 0), (0, 0)))
        return a.reshape(B, n, GLA_CHUNK, H, a.shape[-1]).transpose(1, 0, 3, 2, 4)

    xs = (blocks(q), blocks(k), blocks(v), blocks(log_a))
    causal = jnp.tril(jnp.ones((GLA_CHUNK, GLA_CHUNK), bool))[:, :, None]

    def step(S, inp):
        qc, kc, vc, gc = inp
        b = jnp.cumsum(gc, axis=2)
        diff = b[:, :, :, None, :] - b[:, :, None, :, :]
        decay = jnp.exp(jnp.where(causal, diff, -jnp.inf))
        scores = jnp.einsum('bhik,bhijk,bhjk->bhij', qc, decay, kc)
        o = (jnp.einsum('bhij,bhjv->bhiv', scores, vc)
             + jnp.einsum('bhik,bhkv->bhiv', qc * jnp.exp(b), S))
        b_end = b[:, :, -1, :]
        k_dec = kc * jnp.exp(b_end[:, :, None, :] - b)
        S = jnp.exp(b_end)[..., None] * S + jnp.einsum('bhjk,bhjv->bhkv', k_dec, vc)
        return S, o

    S, o = lax.scan(step, s0, xs)
    o = o.transpose(1, 0, 3, 2, 4).reshape(B, n * GLA_CHUNK, H, HEAD_V)[:, :T]
    return o, S


def _layer(x, pool_hist, gla_state, pos0, g_pre, w_in, w_gate_up, b_gate_up,
           w_pool, pool_scale, g_gla_out, w_out, g_post):
    B, T, _ = x.shape
    h = _rmsnorm(x, g_pre)
    z = h @ w_in.astype(jnp.float32)
    o1 = D_POOL
    o2 = o1 + D_POOL
    o3 = o2 + D_GLA_K
    o4 = o3 + D_GLA_K
    o5 = o4 + D_GLA_V
    o6 = o5 + D_GLA_V
    u_pool, gate_pool = z[..., :o1], z[..., o1:o2]
    q, k, v = z[..., o2:o3], z[..., o3:o4], z[..., o4:o5]
    gate_gla, g_lr = z[..., o5:o6], z[..., o6:]

    u_ext = jnp.concatenate([pool_hist.astype(jnp.float32), u_pool], axis=1)
    y_pool = _pool_mix(u_ext, pos0, w_pool, pool_scale) * jax.nn.silu(gate_pool)

    gk = g_lr @ w_gate_up.astype(jnp.float32) + b_gate_up.astype(jnp.float32)
    log_a = (jax.nn.log_sigmoid(gk) / GATE_NORM).reshape(B, T, N_GLA_HEADS, HEAD_K)
    qh = q.reshape(B, T, N_GLA_HEADS, HEAD_K) * (HEAD_K ** -0.5)
    kh = k.reshape(B, T, N_GLA_HEADS, HEAD_K)
    vh = v.reshape(B, T, N_GLA_HEADS, HEAD_V)
    o, S = _gla(qh, kh, vh, log_a, gla_state.astype(jnp.float32))
    o = _rmsnorm(o, g_gla_out).reshape(B, T, D_GLA_V)
    y_gla = o * jax.nn.silu(gate_gla)

    y = jnp.concatenate([y_pool, y_gla], axis=-1) @ w_out.astype(jnp.float32)
    x_new = (x.astype(jnp.float32) + _rmsnorm(y, g_post)).astype(x.dtype)
    new_hist = u_ext[:, -POOL_HIST:].astype(x.dtype)
    return x_new, new_hist, S.astype(x.dtype)


def setup_inputs(seed: int = 0) -> dict:
    key = jax.random.key(seed)
    ks = jax.random.split(key, 14)
    f32 = jnp.float32
    return {
        "x_prompt": jax.random.normal(ks[0], (BATCH, SEQ, D_MODEL), f32),
        "x_sample": jax.random.normal(ks[1], (DEC_BATCH, DEC_SEQ, D_MODEL), f32),
        "state_pool": jax.random.normal(ks[2], (DEPTH, DEC_BATCH, POOL_HIST, D_POOL), f32),
        "state_gla": 0.5 * jax.random.normal(ks[3], (DEPTH, DEC_BATCH, N_GLA_HEADS, HEAD_K, HEAD_V), f32),
        "g_pre": 1.0 + 0.02 * jax.random.normal(ks[4], (DEPTH, D_MODEL), f32),
        "w_in": jax.random.normal(ks[5], (DEPTH, D_MODEL, D_IN_PROJ), f32) * D_MODEL ** -0.5,
        "w_gate_up": jax.random.normal(ks[6], (DEPTH, GATE_RANK, D_GLA_K), f32) * GATE_RANK ** -0.5,
        "b_gate_up": 0.1 * jax.random.normal(ks[7], (DEPTH, D_GLA_K), f32),
        "w_pool": jax.random.normal(ks[8], (DEPTH, N_POOL_GROUPS, POOL_GROUP, POOL_GROUP), f32) * POOL_GROUP ** -0.5,
        "pool_scale": 1.0 + 0.02 * jax.random.normal(ks[9], (DEPTH, D_POOL), f32),
        "g_gla_out": 1.0 + 0.02 * jax.random.normal(ks[10], (DEPTH, HEAD_V), f32),
        "w_out": jax.random.normal(ks[11], (DEPTH, D_MIX, D_MODEL), f32) * D_MIX ** -0.5,
        "g_post": 1.0 + 0.02 * jax.random.normal(ks[12], (DEPTH, D_MODEL), f32),
    }


def reference(x_prompt, x_sample, state_pool, state_gla, g_pre, w_in, w_gate_up, b_gate_up,
              w_pool, pool_scale, g_gla_out, w_out, g_post):
    hp, hs = x_prompt, x_sample
    pool_p, gla_p, pool_s, gla_s = [], [], [], []
    for l in range(DEPTH):
        w = (g_pre[l], w_in[l], w_gate_up[l], b_gate_up[l], w_pool[l], pool_scale[l],
             g_gla_out[l], w_out[l], g_post[l])
        zero_hist = jnp.zeros((hp.shape[0], POOL_HIST, D_POOL), hp.dtype)
        zero_state = jnp.zeros((hp.shape[0], N_GLA_HEADS, HEAD_K, HEAD_V), jnp.float32)
        hp, ph, ps = _layer(hp, zero_hist, zero_state, 0, *w)
        hs, sh, ss = _layer(hs, state_pool[l], state_gla[l], PAST_LEN, *w)
        pool_p.append(ph)
        gla_p.append(ps)
        pool_s.append(sh)
        gla_s.append(ss)
    new_pool_prompt = jnp.stack(pool_p)
    new_gla_prompt = jnp.stack(gla_p)
    new_pool_sample = jnp.stack(pool_s)
    new_gla_sample = jnp.stack(gla_s)
    return (hp, hs, new_pool_prompt, new_gla_prompt, new_pool_sample, new_gla_sample)
```

```python
import functools

import jax
import jax.numpy as jnp
from jax import lax
from jax.experimental import pallas as pl
from jax.experimental.pallas import tpu as pltpu

D_MODEL = 1024
D_POOL = 1024
POOL_WINDOWS = (2, 4, 8, 16)
POOL_GROUP = D_POOL // len(POOL_WINDOWS)
POOL_HIST = max(POOL_WINDOWS) - 1
HIST_ROWS = 16
D_GLA_K = 512
D_GLA_V = 1024
N_HEADS = 4
HEAD_K = D_GLA_K // N_HEADS
HEAD_V = D_GLA_V // N_HEADS
GATE_RANK = 16
GATE_NORM = 16.0
EPS = 1e-6
PAST_LEN = 1024
COL_U = 0
COL_GP = COL_U + D_POOL
COL_Q = COL_GP + D_POOL
COL_K = COL_Q + D_GLA_K
COL_V = COL_K + D_GLA_K
COL_GG = COL_V + D_GLA_V
COL_LR = COL_GG + D_GLA_V
D_MAIN = COL_LR
N_SLAB = 512
V7X_VMEM_LIMIT_BYTES = 56 * 1024 * 1024

_F32 = jnp.float32
_BF16 = jnp.bfloat16


def _dot(a, b):
    return jnp.dot(a, b, preferred_element_type=_F32)


def _silu(x):
    return x / (1.0 + jnp.exp(-x))


def _log_sigmoid(x):
    return jnp.minimum(x, 0.0) - jnp.log1p(jnp.exp(-jnp.abs(x)))


def _shift_rows(x, sh):
    return pltpu.roll(x, sh, axis=0)


def _cumsum_rows(x):
    n = x.shape[0]
    row = lax.broadcasted_iota(jnp.int32, x.shape, 0)
    sh = 1
    while sh < n:
        x = x + jnp.where(row >= sh, _shift_rows(x, sh), 0.0)
        sh *= 2
    return x


def _layer_kernel(x_ref, hist_ref, s0_ref, g_pre_ref, w_in_ref, w_gate_ref, b_gate_ref, w_pool_ref,
                  pool_scale_ref, g_gla_ref, w_out_ref, g_post_ref,
                  y_ref, hist_out_ref, s_out_ref,
                  h_ref, z_ref, la_ref, mix_ref, hprev_ref, s_ref, *, nb, tb, chunk, pos0):
    t = pl.program_id(1)
    nt = pl.num_programs(1)
    m = nb * tb

    @pl.when(t == 0)
    def _load_state():
        hprev_ref[:, 0:1, :] = jnp.zeros((nb, 1, D_POOL), _F32)
        hprev_ref[:, 1:HIST_ROWS, :] = hist_ref[...]
        s_ref[...] = s0_ref[...]

    x = x_ref[...].reshape(m, D_MODEL)
    ms = jnp.mean(x * x, axis=-1, keepdims=True)
    h_ref[...] = (x * lax.rsqrt(ms + EPS) * g_pre_ref[...]).astype(_BF16)
    for j in range(D_MAIN // N_SLAB):
        cs = slice(j * N_SLAB, (j + 1) * N_SLAB)
        z_ref[:, cs] = _dot(h_ref[...], w_in_ref[:, cs])
    g_lr = _dot(h_ref[...], w_in_ref[:, COL_LR:COL_LR + GATE_RANK])
    gk = _dot(g_lr.astype(_BF16), w_gate_ref[...]) + b_gate_ref[...]
    la_ref[...] = _log_sigmoid(gk) * (1.0 / GATE_NORM)

    pos = pos0 + t * tb + lax.broadcasted_iota(jnp.int32, (tb, 1), 0)
    for g, w in enumerate(POOL_WINDOWS):
        cs = slice(g * POOL_GROUP, (g + 1) * POOL_GROUP)
        cnt = jnp.minimum(w, pos + 1).astype(_F32)
        parts = []
        for n in range(nb):
            rows = slice(n * tb, (n + 1) * tb)
            ext = jnp.concatenate([hprev_ref[n, :, cs], z_ref[rows, cs]], axis=0)
            acc = ext
            sh = 1
            while sh < w:
                acc = acc + _shift_rows(acc, sh)
                sh *= 2
            parts.append(acc[HIST_ROWS:] / cnt - ext[HIST_ROWS:])
        p = parts[0] if nb == 1 else jnp.concatenate(parts, axis=0)
        yp = _dot(p.astype(_BF16), w_pool_ref[g])
        gate = z_ref[:, COL_GP + g * POOL_GROUP:COL_GP + (g + 1) * POOL_GROUP]
        mix_ref[:, cs] = (yp * pool_scale_ref[:, cs] * _silu(gate)).astype(_BF16)

    for n in range(nb):
        hprev_ref[n] = z_ref[(n + 1) * tb - HIST_ROWS:(n + 1) * tb, COL_U:COL_U + D_POOL]

    chunks_per_seq = tb // chunk
    ri = lax.broadcasted_iota(jnp.int32, (chunk, chunk), 0)
    ci = lax.broadcasted_iota(jnp.int32, (chunk, chunk), 1)
    causal = ri >= ci
    q_scale = HEAD_K ** -0.5

    def chunk_body(i, carry):
        r0 = pl.multiple_of(i * chunk, chunk)
        seq = i // chunks_per_seq
        rows = pl.ds(r0, chunk)
        b = _cumsum_rows(la_ref[rows, :])
        eb = jnp.exp(b)
        qe = (z_ref[rows, COL_Q:COL_Q + D_GLA_K] * eb * q_scale).astype(_BF16)
        ke = z_ref[rows, COL_K:COL_K + D_GLA_K] * jnp.exp(-b)
        eb_end = eb[chunk - 1:chunk, :]
        kdec = (ke * eb_end).astype(_BF16)
        ke = ke.astype(_BF16)
        v = z_ref[rows, COL_V:COL_V + D_GLA_V].astype(_BF16)
        for hd in range(N_HEADS):
            ks = slice(hd * HEAD_K, (hd + 1) * HEAD_K)
            vs = slice(hd * HEAD_V, (hd + 1) * HEAD_V)
            sc = lax.dot_general(qe[:, ks], ke[:, ks], (((1,), (1,)), ((), ())),
                                 preferred_element_type=_F32)
            sc = jnp.where(causal, sc, 0.0).astype(_BF16)
            s_old = s_ref[seq, hd]
            o = _dot(sc, v[:, vs]) + _dot(qe[:, ks], s_old.astype(_BF16))
            upd = lax.dot_general(kdec[:, ks], v[:, vs], (((0,), (0,)), ((), ())),
                                  preferred_element_type=_F32)
            e_col = jnp.transpose(jnp.broadcast_to(eb_end[:, ks], (8, HEAD_K)))[:, 0:1]
            s_ref[seq, hd] = s_old * e_col + upd
            on = o * lax.rsqrt(jnp.mean(o * o, axis=-1, keepdims=True) + EPS) * g_gla_ref[...]
            gate = z_ref[rows, COL_GG + hd * HEAD_V:COL_GG + (hd + 1) * HEAD_V]
            mix_ref[rows, D_POOL + hd * HEAD_V:D_POOL + (hd + 1) * HEAD_V] = (on * _silu(gate)).astype(_BF16)
        return carry

    lax.fori_loop(0, m // chunk, chunk_body, 0)

    y = _dot(mix_ref[...], w_out_ref[...])
    ms_y = jnp.mean(y * y, axis=-1, keepdims=True)
    yn = y * lax.rsqrt(ms_y + EPS) * g_post_ref[...]
    y_ref[...] = (x_ref[...].reshape(m, D_MODEL) + yn).reshape(nb, tb, D_MODEL)

    @pl.when(t == nt - 1)
    def _store_state():
        for n in range(nb):
            hist_out_ref[n] = z_ref[(n + 1) * tb - POOL_HIST:(n + 1) * tb, COL_U:COL_U + D_POOL]
        s_out_ref[...] = s_ref[...]


def _const_spec(shape):
    zeros = (0,) * len(shape)
    return pl.BlockSpec(shape, lambda b, t: zeros, pipeline_mode=pl.Buffered(1))


def _layer(x, hist, s0, weights, *, nb, tb, chunk, pos0, name):
    bsz, seq, _ = x.shape
    assert bsz % nb == 0 and seq % tb == 0 and tb % chunk == 0 and tb >= HIST_ROWS and chunk % 8 == 0
    m = nb * tb
    g_pre, w_in, w_gate, b_gate, w_pool, pool_scale, g_gla, w_out, g_post = weights
    kern = functools.partial(_layer_kernel, nb=nb, tb=tb, chunk=chunk, pos0=pos0)
    in_specs = [
        pl.BlockSpec((nb, tb, D_MODEL), lambda b, t: (b, t, 0)),
        pl.BlockSpec((nb, POOL_HIST, D_POOL), lambda b, t: (b, 0, 0)),
        pl.BlockSpec((nb, N_HEADS, HEAD_K, HEAD_V), lambda b, t: (b, 0, 0, 0)),
        _const_spec(g_pre.shape), _const_spec(w_in.shape), _const_spec(w_gate.shape),
        _const_spec(b_gate.shape), _const_spec(w_pool.shape), _const_spec(pool_scale.shape),
        _const_spec(g_gla.shape), _const_spec(w_out.shape), _const_spec(g_post.shape),
    ]
    out_specs = [
        pl.BlockSpec((nb, tb, D_MODEL), lambda b, t: (b, t, 0)),
        pl.BlockSpec((nb, POOL_HIST, D_POOL), lambda b, t: (b, 0, 0)),
        pl.BlockSpec((nb, N_HEADS, HEAD_K, HEAD_V), lambda b, t: (b, 0, 0, 0)),
    ]
    out_shape = [
        jax.ShapeDtypeStruct(x.shape, x.dtype),
        jax.ShapeDtypeStruct((bsz, POOL_HIST, D_POOL), x.dtype),
        jax.ShapeDtypeStruct((bsz, N_HEADS, HEAD_K, HEAD_V), x.dtype),
    ]
    scratch = [
        pltpu.VMEM((m, D_MODEL), _BF16),
        pltpu.VMEM((m, D_MAIN), _F32),
        pltpu.VMEM((m, D_GLA_K), _F32),
        pltpu.VMEM((m, D_POOL + D_GLA_V), _BF16),
        pltpu.VMEM((nb, HIST_ROWS, D_POOL), _F32),
        pltpu.VMEM((nb, N_HEADS, HEAD_K, HEAD_V), _F32),
    ]
    return pl.pallas_call(
        kern,
        grid=(bsz // nb, seq // tb),
        in_specs=in_specs,
        out_specs=out_specs,
        out_shape=out_shape,
        scratch_shapes=scratch,
        compiler_params=pltpu.CompilerParams(
            dimension_semantics=("arbitrary", "arbitrary"),
            vmem_limit_bytes=V7X_VMEM_LIMIT_BYTES),
        name=name,
    )(x, hist, s0, *weights)


def kernel(x_prompt, x_sample, state_pool, state_gla, g_pre, w_in, w_gate_up, b_gate_up, w_pool,
           pool_scale, g_gla_out, w_out, g_post):
    depth = g_pre.shape[0]
    hp, hs = x_prompt, x_sample
    pool_p, gla_p, pool_s, gla_s = [], [], [], []
    for l in range(depth):
        weights = (g_pre[l][None], w_in[l].astype(_BF16), w_gate_up[l].astype(_BF16), b_gate_up[l][None],
                   w_pool[l].astype(_BF16), pool_scale[l][None], g_gla_out[l][None],
                   w_out[l].astype(_BF16), g_post[l][None])
        bp = hp.shape[0]
        zero_hist = jnp.zeros((bp, POOL_HIST, D_POOL), hp.dtype)
        zero_state = jnp.zeros((bp, N_HEADS, HEAD_K, HEAD_V), hp.dtype)
        hp, ph, ps = _layer(hp, zero_hist, zero_state, weights, nb=1, tb=256, chunk=64, pos0=0,
                            name="layer_prompt")
        hs, sh, ss = _layer(hs, state_pool[l], state_gla[l], weights, nb=4, tb=hs.shape[1], chunk=64,
                            pos0=PAST_LEN, name="layer_sample")
        pool_p.append(ph)
        gla_p.append(ps)
        pool_s.append(sh)
        gla_s.append(ss)
    return (hp, hs, jnp.stack(pool_p), jnp.stack(gla_p), jnp.stack(pool_s), jnp.stack(gla_s))
```

```python
import functools

import jax
import jax.numpy as jnp
from jax import lax
from jax.experimental import pallas as pl
from jax.experimental.pallas import tpu as pltpu

D_MODEL = 1024
D_POOL = 1024
POOL_WINDOWS = (2, 4, 8, 16)
POOL_GROUP = D_POOL // len(POOL_WINDOWS)
POOL_HIST = max(POOL_WINDOWS) - 1
HIST_ROWS = 16
D_GLA_K = 512
D_GLA_V = 1024
N_HEADS = 4
HEAD_K = D_GLA_K // N_HEADS
HEAD_V = D_GLA_V // N_HEADS
GATE_RANK = 16
GATE_NORM = 16.0
EPS = 1e-6
PAST_LEN = 1024
COL_U = 0
COL_GP = COL_U + D_POOL
COL_Q = COL_GP + D_POOL
COL_K = COL_Q + D_GLA_K
COL_V = COL_K + D_GLA_K
COL_GG = COL_V + D_GLA_V
COL_LR = COL_GG + D_GLA_V
D_MAIN = COL_LR
N_SLAB = 512
GLA_SAFE_LOG_DECAY = 60.0
EXACT_ROWS = 64
V7X_VMEM_LIMIT_BYTES = 56 * 1024 * 1024

_F32 = jnp.float32
_BF16 = jnp.bfloat16


def _dot(a, b):
    return jnp.dot(a, b, preferred_element_type=_F32)


def _dot_tn(a, b):
    return lax.dot_general(a, b, (((0,), (0,)), ((), ())), preferred_element_type=_F32)


def _dot_nt(a, b):
    return lax.dot_general(a, b, (((1,), (1,)), ((), ())), preferred_element_type=_F32)


def _silu(x):
    return x / (1.0 + jnp.exp(-x))


def _log_sigmoid(x):
    return jnp.minimum(x, 0.0) - jnp.log1p(jnp.exp(-jnp.abs(x)))


def _shift_rows(x, sh):
    return pltpu.roll(x, sh, axis=0)


def _cumsum_rows(x):
    n = x.shape[0]
    row = lax.broadcasted_iota(jnp.int32, x.shape, 0)
    sh = 1
    while sh < n:
        x = x + jnp.where(row >= sh, _shift_rows(x, sh), 0.0)
        sh *= 2
    return x


def _lanes_to_rows(row):
    return jnp.transpose(jnp.broadcast_to(row, (8, HEAD_K)))[:, 0:1]


def _head_store(o, rows, hd, z_ref, mix_ref, g_gla_ref):
    on = o * lax.rsqrt(jnp.mean(o * o, axis=-1, keepdims=True) + EPS) * g_gla_ref[...]
    gate = z_ref[rows, COL_GG + hd * HEAD_V:COL_GG + (hd + 1) * HEAD_V]
    mix_ref[rows, D_POOL + hd * HEAD_V:D_POOL + (hd + 1) * HEAD_V] = (on * _silu(gate)).astype(_BF16)


def _gla_block(n, z_ref, la_ref, mix_ref, g_gla_ref, s_old_ref, s_new_ref, tb):
    rows = slice(n * tb, (n + 1) * tb)
    b = _cumsum_rows(la_ref[rows, :])
    eb = jnp.exp(b)
    qe = (z_ref[rows, COL_Q:COL_Q + D_GLA_K] * eb * HEAD_K ** -0.5).astype(_BF16)
    ke = (z_ref[rows, COL_K:COL_K + D_GLA_K] * jnp.exp(-b)).astype(_BF16)
    eb_end = eb[tb - 1:tb, :]
    v = z_ref[rows, COL_V:COL_V + D_GLA_V].astype(_BF16)
    causal = lax.broadcasted_iota(jnp.int32, (tb, tb), 0) >= lax.broadcasted_iota(jnp.int32, (tb, tb), 1)
    for hd in range(N_HEADS):
        ks = slice(hd * HEAD_K, (hd + 1) * HEAD_K)
        vs = slice(hd * HEAD_V, (hd + 1) * HEAD_V)
        sc = jnp.where(causal, _dot_nt(qe[:, ks], ke[:, ks]), 0.0).astype(_BF16)
        s_old = s_old_ref[n, hd]
        o = _dot(sc, v[:, vs]) + _dot(qe[:, ks], s_old.astype(_BF16))
        s_new_ref[n, hd] = (s_old + _dot_tn(ke[:, ks], v[:, vs])) * _lanes_to_rows(eb_end[:, ks])
        _head_store(o, rows, hd, z_ref, mix_ref, g_gla_ref)
    return jnp.min(b[tb - 1:tb, :])


def _gla_block_exact(n, z_ref, la_ref, b_ref, mix_ref, g_gla_ref, s_ref, tb):
    r = EXACT_ROWS
    row_id = lax.broadcasted_iota(jnp.int32, (r, 1), 0)

    def sub_chunk(c, carry):
        r0 = pl.multiple_of(n * tb + c * r, r)
        rows = pl.ds(r0, r)
        b = _cumsum_rows(la_ref[rows, :])
        b_ref[...] = b
        b_end = b[r - 1:r, :]
        q = z_ref[rows, COL_Q:COL_Q + D_GLA_K] * HEAD_K ** -0.5
        qe = (q * jnp.exp(b)).astype(_BF16)
        kdec = (z_ref[rows, COL_K:COL_K + D_GLA_K] * jnp.exp(b_end - b)).astype(_BF16)
        v = z_ref[rows, COL_V:COL_V + D_GLA_V].astype(_BF16)
        for hd in range(N_HEADS):
            ks = slice(hd * HEAD_K, (hd + 1) * HEAD_K)
            vs = slice(hd * HEAD_V, (hd + 1) * HEAD_V)
            q_h = q[:, ks]
            b_h = b[:, ks]

            def columns(jt, acc, hd=hd, ks=ks, q_h=q_h, b_h=b_h):
                j0 = pl.multiple_of(jt * 8, 8)
                b_t = b_ref[pl.ds(j0, 8), ks]
                k_t = z_ref[pl.ds(r0 + j0, 8), COL_K + hd * HEAD_K:COL_K + (hd + 1) * HEAD_K]
                v_t = z_ref[pl.ds(r0 + j0, 8), COL_V + hd * HEAD_V:COL_V + (hd + 1) * HEAD_V]
                for i in range(8):
                    w = q_h * k_t[i:i + 1] * jnp.exp(jnp.minimum(b_h - b_t[i:i + 1], 0.0))
                    a = jnp.sum(jnp.where(row_id >= j0 + i, w, 0.0), axis=-1, keepdims=True)
                    acc = acc + a * v_t[i:i + 1]
                return acc

            o = lax.fori_loop(0, r // 8, columns, jnp.zeros((r, HEAD_V), _F32))
            s_cur = s_ref[n, hd]
            o = o + _dot(qe[:, ks], s_cur.astype(_BF16))
            s_ref[n, hd] = s_cur * _lanes_to_rows(jnp.exp(b_end[:, ks])) + _dot_tn(kdec[:, ks], v[:, vs])
            _head_store(o, rows, hd, z_ref, mix_ref, g_gla_ref)
        return carry

    lax.fori_loop(0, tb // r, sub_chunk, 0)


def _layer_kernel(x_ref, hist_ref, s0_ref, g_pre_ref, w_in_ref, w_gate_ref, b_gate_ref, w_pool_ref,
                  pool_scale_ref, g_gla_ref, w_out_ref, g_post_ref,
                  y_ref, hist_out_ref, s_out_ref,
                  h_ref, z_ref, la_ref, b_ref, mix_ref, hprev_ref, s_ref, *, nb, tb, pos0):
    t = pl.program_id(1)
    nt = pl.num_programs(1)
    m = nb * tb
    cur = t % 2

    @pl.when(t == 0)
    def _load_state():
        hprev_ref[:, 0:1, :] = jnp.zeros((nb, 1, D_POOL), _F32)
        hprev_ref[:, 1:HIST_ROWS, :] = hist_ref[...]
        s_ref[0] = s0_ref[...]

    x = x_ref[...].reshape(m, D_MODEL)
    ms = jnp.mean(x * x, axis=-1, keepdims=True)
    h_ref[...] = (x * lax.rsqrt(ms + EPS) * g_pre_ref[...]).astype(_BF16)
    for j in range(D_MAIN // N_SLAB):
        cs = slice(j * N_SLAB, (j + 1) * N_SLAB)
        z_ref[:, cs] = _dot(h_ref[...], w_in_ref[:, cs])
    g_lr = _dot(h_ref[...], w_in_ref[:, COL_LR:COL_LR + GATE_RANK])
    gk = _dot(g_lr.astype(_BF16), w_gate_ref[...]) + b_gate_ref[...]
    la_ref[...] = _log_sigmoid(gk) * (1.0 / GATE_NORM)

    pos = pos0 + t * tb + lax.broadcasted_iota(jnp.int32, (tb, 1), 0)
    for g, w in enumerate(POOL_WINDOWS):
        cs = slice(g * POOL_GROUP, (g + 1) * POOL_GROUP)
        cnt = jnp.minimum(w, pos + 1).astype(_F32)
        parts = []
        for n in range(nb):
            rows = slice(n * tb, (n + 1) * tb)
            ext = jnp.concatenate([hprev_ref[n, :, cs], z_ref[rows, cs]], axis=0)
            acc = ext
            sh = 1
            while sh < w:
                acc = acc + _shift_rows(acc, sh)
                sh *= 2
            parts.append(acc[HIST_ROWS:] / cnt - ext[HIST_ROWS:])
        p = parts[0] if nb == 1 else jnp.concatenate(parts, axis=0)
        yp = _dot(p.astype(_BF16), w_pool_ref[g])
        gate = z_ref[:, COL_GP + g * POOL_GROUP:COL_GP + (g + 1) * POOL_GROUP]
        mix_ref[:, cs] = (yp * pool_scale_ref[:, cs] * _silu(gate)).astype(_BF16)

    for n in range(nb):
        hprev_ref[n] = z_ref[(n + 1) * tb - HIST_ROWS:(n + 1) * tb, COL_U:COL_U + D_POOL]

    def finish():
        y = _dot(mix_ref[...], w_out_ref[...])
        ms_y = jnp.mean(y * y, axis=-1, keepdims=True)
        yn = y * lax.rsqrt(ms_y + EPS) * g_post_ref[...]
        y_ref[...] = (x_ref[...].reshape(m, D_MODEL) + yn).reshape(nb, tb, D_MODEL)

    b_min = None
    for n in range(nb):
        b_min_n = _gla_block(n, z_ref, la_ref, mix_ref, g_gla_ref, s_ref.at[cur], s_ref.at[1 - cur], tb)
        b_min = b_min_n if b_min is None else jnp.minimum(b_min, b_min_n)
    finish()

    @pl.when(b_min < -GLA_SAFE_LOG_DECAY)
    def _redo_range_safe():
        s_ref[1 - cur] = s_ref[cur]
        for n in range(nb):
            _gla_block_exact(n, z_ref, la_ref, b_ref, mix_ref, g_gla_ref, s_ref.at[1 - cur], tb)
        finish()

    @pl.when(t == nt - 1)
    def _store_state():
        for n in range(nb):
            hist_out_ref[n] = z_ref[(n + 1) * tb - POOL_HIST:(n + 1) * tb, COL_U:COL_U + D_POOL]
        s_out_ref[...] = s_ref[1 - cur]


def _const_spec(shape):
    zeros = (0,) * len(shape)
    return pl.BlockSpec(shape, lambda b, t: zeros, pipeline_mode=pl.Buffered(1))


def _layer(x, hist, s0, weights, *, nb, tb, pos0, name):
    bsz, seq, _ = x.shape
    assert bsz % nb == 0 and seq % tb == 0 and tb % EXACT_ROWS == 0 and tb >= HIST_ROWS
    m = nb * tb
    g_pre, w_in, w_gate, b_gate, w_pool, pool_scale, g_gla, w_out, g_post = weights
    kern = functools.partial(_layer_kernel, nb=nb, tb=tb, pos0=pos0)
    in_specs = [
        pl.BlockSpec((nb, tb, D_MODEL), lambda b, t: (b, t, 0)),
        pl.BlockSpec((nb, POOL_HIST, D_POOL), lambda b, t: (b, 0, 0)),
        pl.BlockSpec((nb, N_HEADS, HEAD_K, HEAD_V), lambda b, t: (b, 0, 0, 0)),
        _const_spec(g_pre.shape), _const_spec(w_in.shape), _const_spec(w_gate.shape),
        _const_spec(b_gate.shape), _const_spec(w_pool.shape), _const_spec(pool_scale.shape),
        _const_spec(g_gla.shape), _const_spec(w_out.shape), _const_spec(g_post.shape),
    ]
    out_specs = [
        pl.BlockSpec((nb, tb, D_MODEL), lambda b, t: (b, t, 0)),
        pl.BlockSpec((nb, POOL_HIST, D_POOL), lambda b, t: (b, 0, 0)),
        pl.BlockSpec((nb, N_HEADS, HEAD_K, HEAD_V), lambda b, t: (b, 0, 0, 0)),
    ]
    out_shape = [
        jax.ShapeDtypeStruct(x.shape, x.dtype),
        jax.ShapeDtypeStruct((bsz, POOL_HIST, D_POOL), x.dtype),
        jax.ShapeDtypeStruct((bsz, N_HEADS, HEAD_K, HEAD_V), x.dtype),
    ]
    scratch = [
        pltpu.VMEM((m, D_MODEL), _BF16),
        pltpu.VMEM((m, D_MAIN), _F32),
        pltpu.VMEM((m, D_GLA_K), _F32),
        pltpu.VMEM((EXACT_ROWS, D_GLA_K), _F32),
        pltpu.VMEM((m, D_POOL + D_GLA_V), _BF16),
        pltpu.VMEM((nb, HIST_ROWS, D_POOL), _F32),
        pltpu.VMEM((2, nb, N_HEADS, HEAD_K, HEAD_V), _F32),
    ]
    return pl.pallas_call(
        kern,
        grid=(bsz // nb, seq // tb),
        in_specs=in_specs,
        out_specs=out_specs,
        out_shape=out_shape,
        scratch_shapes=scratch,
        compiler_params=pltpu.CompilerParams(
            dimension_semantics=("arbitrary", "arbitrary"),
            vmem_limit_bytes=V7X_VMEM_LIMIT_BYTES),
        name=name,
    )(x, hist, s0, *weights)


def kernel(x_prompt, x_sample, state_pool, state_gla, g_pre, w_in, w_gate_up, b_gate_up, w_pool,
           pool_scale, g_gla_out, w_out, g_post):
    depth = g_pre.shape[0]
    hp, hs = x_prompt, x_sample
    pool_p, gla_p, pool_s, gla_s = [], [], [], []
    for l in range(depth):
        weights = (g_pre[l][None], w_in[l].astype(_BF16), w_gate_up[l].astype(_BF16), b_gate_up[l][None],
                   w_pool[l].astype(_BF16), pool_scale[l][None], g_gla_out[l][None],
                   w_out[l].astype(_BF16), g_post[l][None])
        bp = hp.shape[0]
        zero_hist = jnp.zeros((bp, POOL_HIST, D_POOL), hp.dtype)
        zero_state = jnp.zeros((bp, N_HEADS, HEAD_K, HEAD_V), hp.dtype)
        hp, ph, ps = _layer(hp, zero_hist, zero_state, weights, nb=1, tb=256, pos0=0, name="layer_prompt")
        hs, sh, ss = _layer(hs, state_pool[l], state_gla[l], weights, nb=4, tb=hs.shape[1], pos0=PAST_LEN,
                            name="layer_sample")
        pool_p.append(ph)
        gla_p.append(ps)
        pool_s.append(sh)
        gla_s.append(ss)
    return (hp, hs, jnp.stack(pool_p), jnp.stack(gla_p), jnp.stack(pool_s), jnp.stack(gla_s))
```
